```python
import jax, jax.numpy as jnp
from jax import lax
import numpy as np

D_MODEL = 1024
BATCH = 4
SEQ = 8192
DEPTH = 4
DEC_BATCH = 32
DEC_SEQ = 16
PAST_LEN = 2048

CHUNK = 64
Q_BLOCK = 128
SB_BLOCK = 128
A_HEADS = 4
A_NOPE = 64
A_ROPE = 32
A_V = 128
A_KV_RANK = 256
ROPE_THETA = 10000.0
B_HEADS = 4
B_HEAD_DIM = 64
FORGET_BIAS_INIT = 2.0
C_HEADS = 4
C_HEAD_DIM = 64

A_WIDTH = A_HEADS * A_V
B_WIDTH = B_HEADS * B_HEAD_DIM
C_WIDTH = C_HEADS * C_HEAD_DIM
D_MIX = A_WIDTH + B_WIDTH + C_WIDTH
A_SCALE = (A_NOPE + A_ROPE) ** -0.5
B_SCALE = B_HEAD_DIM ** -0.5
C_SCALE = C_HEAD_DIM ** -0.5
IN_SIZES = (A_HEADS * A_NOPE, A_HEADS * A_ROPE, A_KV_RANK, A_ROPE,
            B_WIDTH, B_WIDTH, B_WIDTH, B_HEADS,
            C_WIDTH, C_WIDTH, C_WIDTH, D_MIX)
IN_COLS = int(sum(IN_SIZES))
IN_SPLITS = tuple(int(v) for v in np.cumsum(IN_SIZES)[:-1])
GROUP_SPLITS = (A_WIDTH, A_WIDTH + B_WIDTH)
NORM_EPS = 1e-6
NEG_INF = -1e30

kernel_name = 'hybrid_mla_fox_stickbreak_stream_step'


def rmsnorm(x, g):
    xf = x.astype(jnp.float32)
    y = xf * lax.rsqrt(jnp.mean(xf * xf, axis=-1, keepdims=True) + NORM_EPS)
    return (y * g.astype(jnp.float32)).astype(x.dtype)


def rope(x, pos):
    half = x.shape[-1] // 2
    inv = ROPE_THETA ** (-jnp.arange(half, dtype=jnp.float32) / half)
    ang = pos.astype(jnp.float32)[:, None] * inv[None, :]
    ang = ang.reshape((ang.shape[0],) + (1,) * (x.ndim - 3) + (half,))
    cos, sin = jnp.cos(ang), jnp.sin(ang)
    xf = x.astype(jnp.float32)
    x1, x2 = xf[..., :half], xf[..., half:]
    return jnp.concatenate([x1 * cos - x2 * sin, x1 * sin + x2 * cos], axis=-1).astype(x.dtype)


def exclusive_suffix_sum(lk):
    L = lk.shape[-1]
    nk = -(-L // SB_BLOCK)
    pad = nk * SB_BLOCK - L
    lkp = jnp.pad(lk, ((0, 0), (0, 0), (0, 0), (0, pad))).reshape(lk.shape[:3] + (nk, SB_BLOCK))
    idx = jnp.arange(SB_BLOCK)
    tri = (idx[:, None] > idx[None, :]).astype(lk.dtype)
    within = jnp.einsum('bhqnj,js->bhqns', lkp, tri)
    bidx = jnp.arange(nk)
    btri = (bidx[:, None] > bidx[None, :]).astype(lk.dtype)
    later = jnp.einsum('bhqm,mn->bhqn', jnp.sum(lkp, axis=-1), btri)
    return (within + later[..., None]).reshape(lk.shape[:3] + (nk * SB_BLOCK,))[..., :L]


def project(h, pos, w_in, kv_norm, forget_bias):
    B, T, _ = h.shape
    (a_qn, a_qr, a_ckv, a_kr, b_q, b_k, b_v, b_f,
     c_q, c_k, c_v, gate) = jnp.split(h @ w_in, IN_SPLITS, axis=-1)
    a_qn = a_qn.reshape(B, T, A_HEADS, A_NOPE)
    a_qr = rope(a_qr.reshape(B, T, A_HEADS, A_ROPE), pos)
    a_ckv = rmsnorm(a_ckv, kv_norm)
    a_kr = rope(a_kr, pos)
    b_q = b_q.reshape(B, T, B_HEADS, B_HEAD_DIM)
    b_k = b_k.reshape(B, T, B_HEADS, B_HEAD_DIM)
    b_v = b_v.reshape(B, T, B_HEADS, B_HEAD_DIM)
    b_logf = jax.nn.log_sigmoid(b_f.astype(jnp.float32) + forget_bias.astype(jnp.float32))
    c_q = c_q.reshape(B, T, C_HEADS, C_HEAD_DIM)
    c_k = c_k.reshape(B, T, C_HEADS, C_HEAD_DIM)
    c_v = c_v.reshape(B, T, C_HEADS, C_HEAD_DIM)
    return (a_qn, a_qr, a_ckv, a_kr, b_q, b_k, b_v, b_logf, c_q, c_k, c_v, gate)


def attend_block(queries, keys, q_pos, k_pos):
    a_qn, a_qr, b_q, b_Fq, c_q = queries
    a_k, a_kr, a_v, b_k, b_v, b_Fk, c_k, c_v = keys
    B, Tq = a_qn.shape[0], a_qn.shape[1]
    s = (jnp.einsum('bqhd,bkhd->bhqk', a_qn, a_k)
         + jnp.einsum('bqhr,bkr->bhqk', a_qr, a_kr)).astype(jnp.float32)
    chunk_ok = (k_pos[None, :] // CHUNK) <= (q_pos[:, None] // CHUNK)
    p = jax.nn.softmax(jnp.where(chunk_ok, s, NEG_INF), axis=-1)
    o_a = jnp.einsum('bhqk,bkhd->bqhd', p.astype(a_v.dtype), a_v)
    causal = k_pos[None, :] <= q_pos[:, None]
    decay = jnp.transpose(b_Fq, (0, 2, 1))[:, :, :, None] - jnp.transpose(b_Fk, (0, 2, 1))[:, :, None, :]
    s = jnp.einsum('bqhd,bkhd->bhqk', b_q, b_k).astype(jnp.float32) + decay
    p = jax.nn.softmax(jnp.where(causal, s, NEG_INF), axis=-1)
    o_b = jnp.einsum('bhqk,bkhd->bqhd', p.astype(b_v.dtype), b_v)
    strict = k_pos[None, :] < q_pos[:, None]
    z = jnp.einsum('bqhd,bkhd->bhqk', c_q, c_k).astype(jnp.float32)
    log_keep = jnp.where(strict, jax.nn.log_sigmoid(-z), 0.0)
    log_after = exclusive_suffix_sum(log_keep)
    w = jnp.exp(jnp.where(strict, z + log_keep + log_after, NEG_INF))
    o_c = jnp.einsum('bhqk,bkhd->bqhd', w.astype(c_v.dtype), c_v)
    return jnp.concatenate([o_a.reshape(B, Tq, A_WIDTH), o_b.reshape(B, Tq, B_WIDTH),
                            o_c.reshape(B, Tq, C_WIDTH)], axis=-1)


def merge_groups(o, gate, out_norm, w_out):
    oa, ob, oc = jnp.split(o, GROUP_SPLITS, axis=-1)
    ga, gb, gc = jnp.split(out_norm, GROUP_SPLITS, axis=-1)
    normed = jnp.concatenate([rmsnorm(oa, ga), rmsnorm(ob, gb), rmsnorm(oc, gc)], axis=-1)
    return (normed * jax.nn.silu(gate)) @ w_out


def layer(x, pos, past, norm_pre, norm_post, w_in, kv_norm, w_uk, w_uv, forget_bias, out_norm, w_out):
    B, T, _ = x.shape
    h = rmsnorm(x, norm_pre)
    (a_qn, a_qr, a_ckv, a_kr, b_q, b_k, b_v, b_logf,
     c_q, c_k, c_v, gate) = project(h, pos, w_in, kv_norm, forget_bias)
    rows = (a_ckv, a_kr, b_k, b_v, b_logf, c_k, c_v)
    if past is None:
        full = rows
        k_pos = pos
    else:
        full = tuple(jnp.concatenate([pc.astype(r.dtype), r], axis=1) for pc, r in zip(past, rows))
        k_pos = jnp.arange(full[0].shape[1], dtype=jnp.int32)
    f_ckv, f_kr, f_bk, f_bv, f_logf, f_ck, f_cv = full
    a_k = jnp.einsum('bkr,rhd->bkhd', f_ckv, w_uk)
    a_v = jnp.einsum('bkr,rhd->bkhd', f_ckv, w_uv)
    F = lax.cumsum(f_logf.astype(jnp.float32), axis=1)
    Fq = F[:, F.shape[1] - T:]
    keys = (a_k, f_kr, a_v, f_bk, f_bv, F, f_ck, f_cv)
    queries = (a_qn * A_SCALE, a_qr * A_SCALE, b_q * B_SCALE, Fq, c_q * C_SCALE)
    if past is None:
        outs = []
        for i in range(T // Q_BLOCK):
            q0, q1 = i * Q_BLOCK, (i + 1) * Q_BLOCK
            outs.append(attend_block(tuple(t[:, q0:q1] for t in queries),
                                     tuple(t[:, :q1] for t in keys), pos[q0:q1], k_pos[:q1]))
        o = jnp.concatenate(outs, axis=1)
    else:
        o = attend_block(queries, keys, pos, k_pos)
    y = merge_groups(o, gate, out_norm, w_out)
    return x + rmsnorm(y, norm_post), rows


def setup_inputs(seed: int = 0) -> dict:
    key = jax.random.key(seed)
    ks = jax.random.split(key, 20)
    f32 = jnp.float32
    nrm = lambda k, shape, scale=1.0: scale * jax.random.normal(k, shape, f32)
    return {
        'x_prompt': nrm(ks[0], (BATCH, SEQ, D_MODEL)),
        'x_sample': nrm(ks[1], (DEC_BATCH, DEC_SEQ, D_MODEL)),
        'cache_mla_ckv': nrm(ks[2], (DEPTH, DEC_BATCH, PAST_LEN, A_KV_RANK)),
        'cache_mla_kpe': nrm(ks[3], (DEPTH, DEC_BATCH, PAST_LEN, A_ROPE)),
        'cache_fox_k': nrm(ks[4], (DEPTH, DEC_BATCH, PAST_LEN, B_HEADS, B_HEAD_DIM)),
        'cache_fox_v': nrm(ks[5], (DEPTH, DEC_BATCH, PAST_LEN, B_HEADS, B_HEAD_DIM)),
        'cache_fox_logf': jax.nn.log_sigmoid(FORGET_BIAS_INIT + nrm(ks[6], (DEPTH, DEC_BATCH, PAST_LEN, B_HEADS))),
        'cache_sb_k': nrm(ks[7], (DEPTH, DEC_BATCH, PAST_LEN, C_HEADS, C_HEAD_DIM)),
        'cache_sb_v': nrm(ks[8], (DEPTH, DEC_BATCH, PAST_LEN, C_HEADS, C_HEAD_DIM)),
        'norm_pre': 1.0 + nrm(ks[9], (DEPTH, D_MODEL), 0.05),
        'norm_post': 1.0 + nrm(ks[10], (DEPTH, D_MODEL), 0.05),
        'w_in': nrm(ks[11], (DEPTH, D_MODEL, IN_COLS), D_MODEL ** -0.5),
        'mla_kv_norm': 1.0 + nrm(ks[12], (DEPTH, A_KV_RANK), 0.05),
        'mla_w_uk': nrm(ks[13], (DEPTH, A_KV_RANK, A_HEADS, A_NOPE), A_KV_RANK ** -0.5),
        'mla_w_uv': nrm(ks[14], (DEPTH, A_KV_RANK, A_HEADS, A_V), A_KV_RANK ** -0.5),
        'fox_forget_bias': FORGET_BIAS_INIT + nrm(ks[15], (DEPTH, B_HEADS), 0.1),
        'out_norm': 1.0 + nrm(ks[16], (DEPTH, D_MIX), 0.05),
        'w_out': nrm(ks[17], (DEPTH, D_MIX, D_MODEL), D_MIX ** -0.5),
    }


def reference(x_prompt, x_sample, cache_mla_ckv, cache_mla_kpe, cache_fox_k, cache_fox_v,
              cache_fox_logf, cache_sb_k, cache_sb_v, norm_pre, norm_post, w_in, mla_kv_norm,
              mla_w_uk, mla_w_uv, fox_forget_bias, out_norm, w_out):
    past_len = cache_mla_ckv.shape[2]
    pos_p = jnp.arange(x_prompt.shape[1], dtype=jnp.int32)
    pos_s = past_len + jnp.arange(x_sample.shape[1], dtype=jnp.int32)
    xp, xs = x_prompt, x_sample
    rows_p, rows_s = [], []
    for l in range(DEPTH):
        wl = (norm_pre[l], norm_post[l], w_in[l], mla_kv_norm[l], mla_w_uk[l], mla_w_uv[l],
              fox_forget_bias[l], out_norm[l], w_out[l])
        xp, rp = layer(xp, pos_p, None, *wl)
        past = (cache_mla_ckv[l], cache_mla_kpe[l], cache_fox_k[l], cache_fox_v[l],
                cache_fox_logf[l], cache_sb_k[l], cache_sb_v[l])
        xs, rs = layer(xs, pos_s, past, *wl)
        rows_p.append(rp)
        rows_s.append(rs)
    st = lambda rows, i: jnp.stack([r[i] for r in rows], axis=0)
    return (xp, xs,
            st(rows_p, 0), st(rows_p, 1), st(rows_p, 2), st(rows_p, 3), st(rows_p, 4), st(rows_p, 5), st(rows_p, 6),
            st(rows_s, 0), st(rows_s, 1), st(rows_s, 2), st(rows_s, 3), st(rows_s, 4), st(rows_s, 5), st(rows_s, 6))
```

```python
import functools
import math

import numpy as np
import jax
import jax.numpy as jnp
from jax import lax
from jax.experimental import pallas as pl
from jax.experimental.pallas import tpu as pltpu

F32 = jnp.float32
BF16 = jnp.bfloat16

D_MODEL = 1024
CHUNK = 64
A_HEADS, A_NOPE, A_ROPE, A_V, A_KV_RANK = 4, 64, 32, 128, 256
B_HEADS, B_HEAD_DIM = 4, 64
C_HEADS, C_HEAD_DIM = 4, 64
ROPE_THETA = 10000.0
A_WIDTH = A_HEADS * A_V
B_WIDTH = B_HEADS * B_HEAD_DIM
C_WIDTH = C_HEADS * C_HEAD_DIM
D_MIX = A_WIDTH + B_WIDTH + C_WIDTH
A_SCALE = (A_NOPE + A_ROPE) ** -0.5
B_SCALE = B_HEAD_DIM ** -0.5
C_SCALE = C_HEAD_DIM ** -0.5
IN_SIZES = (A_HEADS * A_NOPE, A_HEADS * A_ROPE, A_KV_RANK, A_ROPE,
            B_WIDTH, B_WIDTH, B_WIDTH, B_HEADS, C_WIDTH, C_WIDTH, C_WIDTH, D_MIX)
IN_OFFS = tuple(int(v) for v in np.concatenate([[0], np.cumsum(IN_SIZES)]))
NORM_EPS = 1e-6
NEG_INF = -1e30
LOG2E = math.log2(math.e)

LANES = 128
HEAD_PAD = LANES
ROW_TILE = 256
ATT_BLOCK = 512
MERGE_TILE = 512
VMEM_LIMIT = 48 * 1024 * 1024

AUG_NEG_F = B_HEAD_DIM
AUG_ONE = B_HEAD_DIM + 3
GRP_KR = 0
GRP_F = A_ROPE


def _dot(a, b):
    return jnp.dot(a, b, preferred_element_type=F32)


def _dot_nt(a, b):
    return lax.dot_general(a, b, (((1,), (1,)), ((), ())), preferred_element_type=F32)


def _split2(x):
    hi = x.astype(BF16)
    lo = (x - hi.astype(F32)).astype(BF16)
    return hi, lo


def _split3(x):
    h1 = x.astype(BF16)
    r1 = x - h1.astype(F32)
    h2 = r1.astype(BF16)
    h3 = (r1 - h2.astype(F32)).astype(BF16)
    return h1, h2, h3


def _div_pow2(x, n):
    assert n & (n - 1) == 0
    return jnp.right_shift(x, n.bit_length() - 1)


def _mod_pow2(x, n):
    assert n & (n - 1) == 0
    return jnp.bitwise_and(x, n - 1)


def _softplus(z):
    return jnp.maximum(z, 0.0) + jnp.log1p(jnp.exp(-jnp.abs(z)))


def _rope_lanes(x, cos, sin_lo, sin_hi):
    return x * cos + pltpu.roll(x, LANES - A_ROPE // 2, 1) * sin_lo + pltpu.roll(x, A_ROPE // 2, 1) * sin_hi


def _proj_kernel(x_ref, npre_ref, wtok_ref, wvt_ref, kvn_ref, wukp_ref, wuvt_ref, sel_ref, place_ref, fb_ref,
                 cosa_ref, sla_ref, sha_ref, cosg_ref, slg_ref, shg_ref,
                 ckv_o, kpe_o, bk_o, bv_o, logf_o, ck_o, cv_o, grp_o,
                 qa_o, ka_o, qb_o, kb_o, qc_o, kc_o, qbu_o, qcu_o, gate_o, vta_o, vtbc_o):
    x = x_ref[...]
    ms = jnp.mean(x * x, axis=-1, keepdims=True)
    h = (x * lax.rsqrt(ms + NORM_EPS) * npre_ref[...]).astype(BF16)

    col = [0]

    def mm(width):
        a = col[0]
        col[0] = a + width
        return _dot(h, wtok_ref[:, a:a + width])

    qa = mm(A_HEADS * HEAD_PAD) * (A_SCALE * LOG2E)
    cosa, sla, sha = cosa_ref[...], sla_ref[...], sha_ref[...]
    qa_o[...] = jnp.concatenate(
        [_rope_lanes(qa[:, g * HEAD_PAD:(g + 1) * HEAD_PAD], cosa, sla, sha) for g in range(A_HEADS)],
        axis=1).astype(BF16)

    ckv = mm(A_KV_RANK)
    ckvn = ckv * lax.rsqrt(jnp.mean(ckv * ckv, axis=-1, keepdims=True) + NORM_EPS) * kvn_ref[...]
    ckv_o[...] = ckvn
    cb = ckvn.astype(BF16)

    grp = mm(LANES)
    roped = _rope_lanes(grp, cosg_ref[...], slg_ref[...], shg_ref[...])
    xf = grp + fb_ref[...]
    logsig = jnp.minimum(xf, 0.0) - jnp.log1p(jnp.exp(-jnp.abs(xf)))
    lane = lax.broadcasted_iota(jnp.int32, grp.shape, 1)
    grp2 = jnp.where(lane < GRP_F, roped, jnp.where(lane < GRP_F + B_HEADS, logsig, 0.0))
    grp_o[...] = grp2
    kpe_o[...] = grp2[:, GRP_KR:GRP_KR + A_ROPE]
    logf_o[...] = grp2[:, GRP_F:GRP_F + B_HEADS]

    ka_o[...] = (_dot(cb, wukp_ref[...]) + _dot(grp2.astype(BF16), sel_ref[...])).astype(BF16)

    bq = (mm(B_WIDTH) * (B_SCALE * LOG2E)).astype(BF16)
    bk = mm(B_WIDTH)
    bv = mm(B_WIDTH)
    cq = (mm(C_WIDTH) * C_SCALE).astype(BF16)
    ck = mm(C_WIDTH)
    cv = mm(C_WIDTH)
    gate = mm(D_MIX)
    bk_o[...] = bk
    bv_o[...] = bv
    ck_o[...] = ck
    cv_o[...] = cv
    qbu_o[...] = bq
    qcu_o[...] = cq
    place = place_ref[...]
    qb_o[...] = _dot(bq, place).astype(BF16)
    kb_o[...] = _dot(bk.astype(BF16), place).astype(BF16)
    qc_o[...] = _dot(cq, place).astype(BF16)
    kc_o[...] = _dot(ck.astype(BF16), place).astype(BF16)
    gate_o[...] = gate.astype(BF16)

    vta_o[0] = _dot_nt(wuvt_ref[...], cb).astype(BF16)
    vtbc_o[0] = _dot_nt(wvt_ref[...], h).astype(BF16)


def _project(x2d, wl, tabs, n_tab_tiles):
    n = x2d.shape[0]
    tm = ROW_TILE
    nt = n // tm
    row = lambda w: pl.BlockSpec((tm, w), lambda i: (i, 0))
    full = lambda a: pl.BlockSpec(a.shape, lambda i: (0,) * a.ndim)
    tab = pl.BlockSpec((tm, LANES), lambda i: (i % n_tab_tiles, 0))
    weights = (wl['npre'], wl['wtok'], wl['wvt'], wl['kvn'], wl['wukp'], wl['wuvt'], wl['sel'], wl['place'], wl['fb'])
    in_specs = [row(D_MODEL)] + [full(w) for w in weights] + [tab] * 6
    f32o = lambda w: jax.ShapeDtypeStruct((n, w), F32)
    bfo = lambda w: jax.ShapeDtypeStruct((n, w), BF16)
    wide = A_HEADS * HEAD_PAD
    out_shape = (f32o(A_KV_RANK), f32o(A_ROPE), f32o(B_WIDTH), f32o(B_WIDTH), f32o(B_HEADS), f32o(C_WIDTH),
                 f32o(C_WIDTH), f32o(LANES),
                 bfo(wide), bfo(wide), bfo(wide), bfo(wide), bfo(wide), bfo(wide), bfo(B_WIDTH), bfo(C_WIDTH),
                 bfo(D_MIX),
                 jax.ShapeDtypeStruct((nt, A_WIDTH, tm), BF16),
                 jax.ShapeDtypeStruct((nt, B_WIDTH + C_WIDTH, tm), BF16))
    out_specs = ([row(s.shape[1]) for s in out_shape[:17]]
                 + [pl.BlockSpec((1, A_WIDTH, tm), lambda i: (i, 0, 0)),
                    pl.BlockSpec((1, B_WIDTH + C_WIDTH, tm), lambda i: (i, 0, 0))])
    return pl.pallas_call(
        _proj_kernel, grid=(nt,), in_specs=in_specs, out_specs=out_specs, out_shape=out_shape,
        compiler_params=pltpu.CompilerParams(dimension_semantics=("parallel",), vmem_limit_bytes=VMEM_LIMIT),
        name="proj",
    )(x2d, *weights, *tabs)


def _fprep_kernel(grp_ref, qb_ref, kb_ref, tril_ref, pk_ref, pq_ref, onek_ref, oneq_ref, qb_o, kb_o, carry_sc):
    @pl.when(pl.program_id(1) == 0)
    def _():
        carry_sc[...] = jnp.zeros_like(carry_sc)

    g = grp_ref[...]
    lane = lax.broadcasted_iota(jnp.int32, g.shape, 1)
    g = jnp.where((lane >= GRP_F) & (lane < GRP_F + B_HEADS), g, 0.0)
    tril = tril_ref[...]
    g1, g2, g3 = _split3(g)
    f = _dot(tril, g1) + _dot(tril, g2) + _dot(tril, g3) + carry_sc[0:1, :]
    ts = g.shape[0]
    carry_sc[...] = jnp.broadcast_to(f[ts - 1:ts, :], carry_sc.shape)
    f1, f2, f3 = _split3(f * LOG2E)
    kb_o[...] = (kb_ref[...].astype(F32) + _dot(f1, pk_ref[0]) + _dot(f2, pk_ref[1]) + _dot(f3, pk_ref[2])
                 + onek_ref[...]).astype(BF16)
    qb_o[...] = (qb_ref[...].astype(F32) + _dot(f1, pq_ref[0]) + _dot(f2, pq_ref[1]) + _dot(f3, pq_ref[2])
                 + oneq_ref[...]).astype(BF16)


def _fox_augment(grp, qb, kb, consts, batch, seq):
    ts = ROW_TILE
    nt = seq // ts
    wide = B_HEADS * HEAD_PAD
    row = lambda w: pl.BlockSpec((ts, w), lambda b, t: (b * nt + t, 0))
    full = lambda a: pl.BlockSpec(a.shape, lambda b, t: (0,) * a.ndim)
    cs = (consts['tril'], consts['pk'], consts['pq'], consts['onek'], consts['oneq'])
    return pl.pallas_call(
        _fprep_kernel, grid=(batch, nt),
        in_specs=[row(LANES), row(wide), row(wide)] + [full(c) for c in cs],
        out_specs=[row(wide), row(wide)],
        out_shape=(jax.ShapeDtypeStruct(qb.shape, BF16), jax.ShapeDtypeStruct(kb.shape, BF16)),
        scratch_shapes=[pltpu.VMEM((8, LANES), F32)],
        compiler_params=pltpu.CompilerParams(dimension_semantics=("parallel", "arbitrary"),
                                             vmem_limit_bytes=VMEM_LIMIT),
        name="fox_augment",
    )(grp, qb, kb, *cs)


def _diag_iotas(blk):
    kk = lax.broadcasted_iota(jnp.int32, (blk, blk), 0)
    qq = lax.broadcasted_iota(jnp.int32, (blk, blk), 1)
    return kk, qq


def _softmax_step(q, k_ref, vt_ref, m_sc, l_sc, acc_sc, j, lane0, row0, nrow, mask):
    blk = ATT_BLOCK
    k = k_ref[0, pl.ds(pl.multiple_of(j * blk, blk), blk), lane0:lane0 + HEAD_PAD]
    st = _dot_nt(k, q)
    if mask is not None:
        st = jnp.where(mask, st, NEG_INF)
    hs = slice(row0 // nrow, row0 // nrow + 1)
    m_old = m_sc[hs, :]
    m_new = jnp.maximum(m_old, jnp.max(st, axis=0, keepdims=True))
    alpha = jnp.exp2(m_old - m_new)
    p = jnp.exp2(st - m_new)
    l_sc[hs, :] = alpha * l_sc[hs, :] + jnp.sum(p, axis=0, keepdims=True)
    pb = p.astype(BF16)
    nslab = blk // ROW_TILE
    pv = None
    for s in range(nslab):
        d = _dot(vt_ref[0, j * nslab + s, row0:row0 + nrow, :], pb[s * ROW_TILE:(s + 1) * ROW_TILE, :])
        pv = d if pv is None else pv + d
    acc_sc[row0:row0 + nrow, :] = alpha * acc_sc[row0:row0 + nrow, :] + pv
    m_sc[hs, :] = m_new


def _attn_softmax_kernel(q_ref, k_ref, vt_ref, o_ref, m_sc, l_sc, acc_sc, *, heads, dv, causal_kind):
    qi = pl.program_id(2)
    m_sc[...] = jnp.full(m_sc.shape, NEG_INF, F32)
    l_sc[...] = jnp.zeros_like(l_sc)
    acc_sc[...] = jnp.zeros_like(acc_sc)
    kk, qq = _diag_iotas(ATT_BLOCK)
    if causal_kind == "chunk":
        dmask = _div_pow2(kk, CHUNK) <= _div_pow2(qq, CHUNK)
    else:
        dmask = kk <= qq
    qs = [q_ref[0, :, hh * HEAD_PAD:(hh + 1) * HEAD_PAD] for hh in range(heads)]

    def body(j, c):
        for hh in range(heads):
            _softmax_step(qs[hh], k_ref, vt_ref, m_sc, l_sc, acc_sc, j, hh * HEAD_PAD, hh * dv, dv, None)
        return c

    lax.fori_loop(0, qi, body, 0)
    for hh in range(heads):
        _softmax_step(qs[hh], k_ref, vt_ref, m_sc, l_sc, acc_sc, qi, hh * HEAD_PAD, hh * dv, dv, dmask)
    outs = []
    for hh in range(heads):
        outs.append(acc_sc[hh * dv:(hh + 1) * dv, :] / l_sc[hh:hh + 1, :])
    o_ref[0] = jnp.concatenate(outs, axis=0).T


def _attn_softmax(q, k, vt, *, batch, seq, heads_per_step, dv, vt_row_block0, causal_kind, name):
    blk = ATT_BLOCK
    nq = seq // blk
    ns = seq // ROW_TILE
    groups = 4 // heads_per_step
    hw = heads_per_step * HEAD_PAD
    rows = heads_per_step * dv
    kern = functools.partial(_attn_softmax_kernel, heads=heads_per_step, dv=dv, causal_kind=causal_kind)
    return pl.pallas_call(
        kern, grid=(batch, groups, nq),
        in_specs=[pl.BlockSpec((1, blk, hw), lambda b, g, i: (b, i, g)),
                  pl.BlockSpec((1, seq, hw), lambda b, g, i: (b, 0, g)),
                  pl.BlockSpec((1, ns, rows, ROW_TILE), lambda b, g, i: (b, 0, vt_row_block0 + g, 0))],
        out_specs=pl.BlockSpec((1, blk, rows), lambda b, g, i: (b, i, g)),
        out_shape=jax.ShapeDtypeStruct((batch, seq, 4 * dv), F32),
        scratch_shapes=[pltpu.VMEM((heads_per_step, blk), F32), pltpu.VMEM((heads_per_step, blk), F32),
                        pltpu.VMEM((rows, blk), F32)],
        compiler_params=pltpu.CompilerParams(dimension_semantics=("parallel", "parallel", "arbitrary"),
                                             vmem_limit_bytes=VMEM_LIMIT),
        name=name,
    )(q, k, vt)


def _stick_step(q, k_ref, vt_ref, tri, carry_sc, acc_sc, j, hh, mask):
    blk = ATT_BLOCK
    dv = C_HEAD_DIM
    k = k_ref[0, pl.ds(pl.multiple_of(j * blk, blk), blk), hh * HEAD_PAD:(hh + 1) * HEAD_PAD]
    zt = _dot_nt(k, q)
    lk = -_softplus(zt)
    if mask is not None:
        lk = jnp.where(mask, lk, 0.0)
    hi, lo = _split2(lk)
    cum = _dot(tri, hi) + _dot(tri, lo)
    carry = carry_sc[hh:hh + 1, :]
    w = jnp.exp(zt + cum + carry)
    if mask is not None:
        w = jnp.where(mask, w, 0.0)
    wb = w.astype(BF16)
    nslab = blk // ROW_TILE
    pv = None
    for s in range(nslab):
        d = _dot(vt_ref[0, j * nslab + s, hh * dv:(hh + 1) * dv, :], wb[s * ROW_TILE:(s + 1) * ROW_TILE, :])
        pv = d if pv is None else pv + d
    acc_sc[hh * dv:(hh + 1) * dv, :] += pv
    carry_sc[hh:hh + 1, :] = carry + cum[0:1, :]


def _attn_stick_kernel(q_ref, k_ref, vt_ref, tri_ref, o_ref, carry_sc, acc_sc, *, heads):
    qi = pl.program_id(2)
    carry_sc[...] = jnp.zeros_like(carry_sc)
    acc_sc[...] = jnp.zeros_like(acc_sc)
    kk, qq = _diag_iotas(ATT_BLOCK)
    dmask = kk < qq
    tri = tri_ref[...]
    qs = [q_ref[0, :, hh * HEAD_PAD:(hh + 1) * HEAD_PAD] for hh in range(heads)]
    for hh in range(heads):
        _stick_step(qs[hh], k_ref, vt_ref, tri, carry_sc, acc_sc, qi, hh, dmask)

    def body(t, c):
        j = qi - 1 - t
        for hh in range(heads):
            _stick_step(qs[hh], k_ref, vt_ref, tri, carry_sc, acc_sc, j, hh, None)
        return c

    lax.fori_loop(0, qi, body, 0)
    o_ref[0] = acc_sc[...].T


def _attn_stick(q, k, vt, tri, *, batch, seq, vt_row_block0):
    blk = ATT_BLOCK
    nq = seq // blk
    ns = seq // ROW_TILE
    heads = 2
    hw = heads * HEAD_PAD
    rows = heads * C_HEAD_DIM
    kern = functools.partial(_attn_stick_kernel, heads=heads)
    return pl.pallas_call(
        kern, grid=(batch, C_HEADS // heads, nq),
        in_specs=[pl.BlockSpec((1, blk, hw), lambda b, g, i: (b, i, g)),
                  pl.BlockSpec((1, seq, hw), lambda b, g, i: (b, 0, g)),
                  pl.BlockSpec((1, ns, rows, ROW_TILE), lambda b, g, i: (b, 0, vt_row_block0 + g, 0)),
                  pl.BlockSpec(tri.shape, lambda b, g, i: (0, 0))],
        out_specs=pl.BlockSpec((1, blk, rows), lambda b, g, i: (b, i, g)),
        out_shape=jax.ShapeDtypeStruct((batch, seq, C_WIDTH), F32),
        scratch_shapes=[pltpu.VMEM((heads, blk), F32), pltpu.VMEM((rows, blk), F32)],
        compiler_params=pltpu.CompilerParams(dimension_semantics=("parallel", "parallel", "arbitrary"),
                                             vmem_limit_bytes=VMEM_LIMIT),
        name="attn_stick",
    )(q, k, vt, tri)


def _merge_kernel(oa_ref, ob_ref, oc_ref, gate_ref, onorm_ref, wout_ref, x_ref, npost_ref, xo_ref):
    def gnorm(o):
        return o * lax.rsqrt(jnp.mean(o * o, axis=-1, keepdims=True) + NORM_EPS)

    normed = jnp.concatenate([gnorm(oa_ref[...]), gnorm(ob_ref[...]), gnorm(oc_ref[...])], axis=1) * onorm_ref[...]
    g = gate_ref[...].astype(F32)
    act = g / (1.0 + jnp.exp(-g))
    y = _dot((normed * act).astype(BF16), wout_ref[...])
    yn = y * lax.rsqrt(jnp.mean(y * y, axis=-1, keepdims=True) + NORM_EPS) * npost_ref[...]
    xo_ref[...] = x_ref[...] + yn


def _merge(oa, ob, oc, gate, wl, x2d):
    n = x2d.shape[0]
    tm = min(MERGE_TILE, n)
    row = lambda w: pl.BlockSpec((tm, w), lambda i: (i, 0))
    full = lambda a: pl.BlockSpec(a.shape, lambda i: (0,) * a.ndim)
    return pl.pallas_call(
        _merge_kernel, grid=(n // tm,),
        in_specs=[row(A_WIDTH), row(B_WIDTH), row(C_WIDTH), row(D_MIX), full(wl['onorm']), full(wl['wout']),
                  row(D_MODEL), full(wl['npost'])],
        out_specs=row(D_MODEL),
        out_shape=jax.ShapeDtypeStruct((n, D_MODEL), F32),
        compiler_params=pltpu.CompilerParams(dimension_semantics=("parallel",), vmem_limit_bytes=VMEM_LIMIT),
        name="merge",
    )(oa, ob, oc, gate, wl['onorm'], wl['wout'], x2d, wl['npost'])


def _cumsum_rows_kernel(x_ref, triu_ref, f_ref):
    blkw = triu_ref.shape[0]
    triu = triu_ref[...]
    carry = jnp.zeros((x_ref.shape[0], 1), F32)
    for c in range(x_ref.shape[1] // blkw):
        x1, x2, x3 = _split3(x_ref[:, c * blkw:(c + 1) * blkw])
        pref = _dot(x1, triu) + _dot(x2, triu) + _dot(x3, triu)
        f_ref[:, c * blkw:(c + 1) * blkw] = pref + carry
        carry = carry + pref[:, blkw - 1:blkw]


def _cumsum_rows(x, triu):
    return pl.pallas_call(
        _cumsum_rows_kernel, grid=(1,),
        in_specs=[pl.BlockSpec(x.shape, lambda i: (0, 0)), pl.BlockSpec(triu.shape, lambda i: (0, 0))],
        out_specs=pl.BlockSpec(x.shape, lambda i: (0, 0)),
        out_shape=jax.ShapeDtypeStruct(x.shape, F32),
        compiler_params=pltpu.CompilerParams(vmem_limit_bytes=VMEM_LIMIT),
        name="cumsum_rows",
    )(x, triu)


def _pad_rows(a, n):
    return jnp.concatenate([a, jnp.zeros((n - a.shape[0], a.shape[1]), a.dtype)], axis=0)


def _sample_attn_kernel(qa_ref, qbu_ref, qcu_ref, fq_ref, fk_ref,
                        ckv_c, kpe_c, fk_c, fv_c, sk_c, sv_c,
                        ckv_n, kpe_n, bk_n, bv_n, ck_n, cv_n,
                        mabs_ref, wuv_ref, trit_ref,
                        oa_ref, ob_ref, oc_ref, kpc_sc, kpn_sc, *, past, t_new):
    heads = A_HEADS
    rows = heads * t_new
    nb = LANES
    row_i = lax.broadcasted_iota(jnp.int32, (rows, nb), 0)
    col_i = lax.broadcasted_iota(jnp.int32, (rows, nb), 1)
    q_t = _mod_pow2(row_i, t_new)
    valid_n = col_i < t_new
    q_pos = past + q_t
    k_pos = past + col_i

    qa = qa_ref[0]
    qcat = jnp.concatenate([_dot(qa[:, h * HEAD_PAD:(h + 1) * HEAD_PAD], mabs_ref[h]) for h in range(heads)], axis=0)
    qabs = qcat[:, :A_KV_RANK].astype(BF16)
    qrp = qcat[:, A_KV_RANK:].astype(BF16)
    ckvc = ckv_c[0].astype(BF16)
    kpc_sc[...] = jnp.zeros_like(kpc_sc)
    kpc_sc[:, 0:A_ROPE] = kpe_c[0]
    kpn_sc[...] = jnp.zeros_like(kpn_sc)
    kpn_sc[0:t_new, 0:A_ROPE] = kpe_n[0]
    ckvn = _pad_rows(ckv_n[0], nb).astype(BF16)
    s_c = _dot_nt(qabs, ckvc) + _dot_nt(qrp, kpc_sc[...].astype(BF16))
    s_n = _dot_nt(qabs, ckvn) + _dot_nt(qrp, kpn_sc[...].astype(BF16))
    s_n = jnp.where(valid_n & (_div_pow2(k_pos, CHUNK) <= _div_pow2(q_pos, CHUNK)), s_n, NEG_INF)
    m = jnp.maximum(jnp.max(s_c, axis=1, keepdims=True), jnp.max(s_n, axis=1, keepdims=True))
    p_c = jnp.exp2(s_c - m)
    p_n = jnp.exp2(s_n - m)
    l = jnp.sum(p_c, axis=1, keepdims=True) + jnp.sum(p_n, axis=1, keepdims=True)
    olat = ((_dot(p_c.astype(BF16), ckvc) + _dot(p_n.astype(BF16), ckvn)) / l).astype(BF16)
    oa_ref[0] = jnp.concatenate([_dot(olat[h * t_new:(h + 1) * t_new, :], wuv_ref[h]) for h in range(heads)], axis=1)

    rw = lax.broadcasted_iota(jnp.int32, (rows, B_WIDTH), 0)
    lw = lax.broadcasted_iota(jnp.int32, (rows, B_WIDTH), 1)
    head_sel = _div_pow2(rw, t_new) == _div_pow2(lw, B_HEAD_DIM)

    def block_diag(qu):
        return jnp.where(head_sel, jnp.concatenate([qu] * heads, axis=0), jnp.zeros((), qu.dtype))

    def pick_heads(ofull):
        lt = _div_pow2(lax.broadcasted_iota(jnp.int32, (t_new, B_WIDTH), 1), B_HEAD_DIM)
        out = jnp.zeros((t_new, B_WIDTH), F32)
        for h in range(heads):
            out = out + jnp.where(lt == h, ofull[h * t_new:(h + 1) * t_new, :], 0.0)
        return out

    qbd = block_diag(qbu_ref[0])
    fk = fk_ref[0]
    fk_rows = jnp.concatenate([jnp.broadcast_to(fk[h:h + 1, :], (t_new, fk.shape[1])) for h in range(heads)], axis=0)
    fq = fq_ref[0]
    fkc = fk_c[0].astype(BF16)
    bkn = _pad_rows(bk_n[0], nb).astype(BF16)
    s_c = _dot_nt(qbd, fkc) + (fq - fk_rows[:, :past]) * LOG2E
    s_n = _dot_nt(qbd, bkn) + (fq - fk_rows[:, past:past + nb]) * LOG2E
    s_n = jnp.where(valid_n & (k_pos <= q_pos), s_n, NEG_INF)
    m = jnp.maximum(jnp.max(s_c, axis=1, keepdims=True), jnp.max(s_n, axis=1, keepdims=True))
    p_c = jnp.exp2(s_c - m)
    p_n = jnp.exp2(s_n - m)
    l = jnp.sum(p_c, axis=1, keepdims=True) + jnp.sum(p_n, axis=1, keepdims=True)
    ofull = (_dot(p_c.astype(BF16), fv_c[0].astype(BF16))
             + _dot(p_n.astype(BF16), _pad_rows(bv_n[0], nb).astype(BF16))) / l
    ob_ref[0] = pick_heads(ofull)

    qcd = block_diag(qcu_ref[0])
    trit = trit_ref[...]
    tw = trit.shape[0]
    z_c = _dot_nt(qcd, sk_c[0].astype(BF16))
    z_n = _dot_nt(qcd, _pad_rows(ck_n[0], nb).astype(BF16))
    strict_n = valid_n & (k_pos < q_pos)
    lk_n = jnp.where(strict_n, -_softplus(z_n), 0.0)
    hi, lo = _split2(lk_n)
    cum_n = _dot(hi, trit[:nb, :nb]) + _dot(lo, trit[:nb, :nb])
    w_n = jnp.where(strict_n, jnp.exp(z_n + cum_n), 0.0)
    lk_c = -_softplus(z_c)
    ncb = past // tw
    cums = []
    for c in range(ncb):
        hi, lo = _split2(lk_c[:, c * tw:(c + 1) * tw])
        cums.append(_dot(hi, trit) + _dot(lo, trit))
    carry = cum_n[:, 0:1]
    w_blocks = [None] * ncb
    for c in range(ncb - 1, -1, -1):
        w_blocks[c] = jnp.exp(z_c[:, c * tw:(c + 1) * tw] + cums[c] + carry)
        carry = carry + cums[c][:, 0:1]
    w_c = jnp.concatenate(w_blocks, axis=1)
    ofull = (_dot(w_c.astype(BF16), sv_c[0].astype(BF16))
             + _dot(w_n.astype(BF16), _pad_rows(cv_n[0], nb).astype(BF16)))
    oc_ref[0] = pick_heads(ofull)


def _sample_attn(qa, qbu, qcu, fq, fk, caches, news, wl, trit, *, dec_batch, t_new, past):
    def blk3(a):
        return pl.BlockSpec((1,) + a.shape[1:], lambda b: (b, 0, 0))

    full = lambda a: pl.BlockSpec(a.shape, lambda b: (0,) * a.ndim)
    ins = [qa, qbu, qcu, fq, fk] + list(caches) + list(news)
    consts = [wl['mabs'], wl['wuv'], trit]
    kern = functools.partial(_sample_attn_kernel, past=past, t_new=t_new)
    return pl.pallas_call(
        kern, grid=(dec_batch,),
        in_specs=[blk3(a) for a in ins] + [full(c) for c in consts],
        out_specs=[pl.BlockSpec((1, t_new, A_WIDTH), lambda b: (b, 0, 0)),
                   pl.BlockSpec((1, t_new, B_WIDTH), lambda b: (b, 0, 0)),
                   pl.BlockSpec((1, t_new, C_WIDTH), lambda b: (b, 0, 0))],
        out_shape=(jax.ShapeDtypeStruct((dec_batch, t_new, A_WIDTH), F32),
                   jax.ShapeDtypeStruct((dec_batch, t_new, B_WIDTH), F32),
                   jax.ShapeDtypeStruct((dec_batch, t_new, C_WIDTH), F32)),
        scratch_shapes=[pltpu.VMEM((past, LANES), F32), pltpu.VMEM((LANES, LANES), F32)],
        compiler_params=pltpu.CompilerParams(dimension_semantics=("parallel",), vmem_limit_bytes=VMEM_LIMIT),
        name="sample_attn",
    )(*ins, *consts)


def _rope_tables(pos):
    half = A_ROPE // 2
    inv = ROPE_THETA ** (-jnp.arange(half, dtype=F32) / half)
    ang = pos.astype(F32)[:, None] * inv[None, :]
    cos, sin = jnp.cos(ang), jnp.sin(ang)
    n = pos.shape[0]
    z = lambda w: jnp.zeros((n, w), F32)
    o = lambda w: jnp.ones((n, w), F32)
    cosa = jnp.concatenate([o(A_NOPE), cos, cos, o(HEAD_PAD - A_NOPE - A_ROPE)], axis=1)
    sla = jnp.concatenate([z(A_NOPE), -sin, z(half), z(HEAD_PAD - A_NOPE - A_ROPE)], axis=1)
    sha = jnp.concatenate([z(A_NOPE), z(half), sin, z(HEAD_PAD - A_NOPE - A_ROPE)], axis=1)
    cosg = jnp.concatenate([cos, cos, o(LANES - A_ROPE)], axis=1)
    slg = jnp.concatenate([-sin, z(half), z(LANES - A_ROPE)], axis=1)
    shg = jnp.concatenate([z(half), sin, z(LANES - A_ROPE)], axis=1)
    return cosa, sla, sha, cosg, slg, shg


def _constants():
    wide = 4 * HEAD_PAD
    place = np.zeros((B_WIDTH, wide), np.float32)
    for h in range(4):
        for d in range(B_HEAD_DIM):
            place[h * B_HEAD_DIM + d, h * HEAD_PAD + d] = 1.0
    sel = np.zeros((LANES, wide), np.float32)
    for h in range(A_HEADS):
        for j in range(A_ROPE):
            sel[GRP_KR + j, h * HEAD_PAD + A_NOPE + j] = 1.0
    pk = np.zeros((3, LANES, wide), np.float32)
    pq = np.zeros((3, LANES, wide), np.float32)
    onek = np.zeros((1, wide), np.float32)
    oneq = np.zeros((1, wide), np.float32)
    for h in range(B_HEADS):
        for i in range(3):
            pk[i, GRP_F + h, h * HEAD_PAD + AUG_NEG_F + i] = -1.0
            pq[i, GRP_F + h, h * HEAD_PAD + AUG_ONE + i] = 1.0
            onek[0, h * HEAD_PAD + AUG_ONE + i] = 1.0
            oneq[0, h * HEAD_PAD + AUG_NEG_F + i] = 1.0
    r = np.arange(ROW_TILE)
    tril = (r[:, None] >= r[None, :]).astype(np.float32)
    a = np.arange(ATT_BLOCK)
    tri_suffix = (a[None, :] >= a[:, None]).astype(np.float32)
    return dict(place=jnp.asarray(place, BF16), sel=jnp.asarray(sel, BF16),
                pk=jnp.asarray(pk, BF16), pq=jnp.asarray(pq, BF16),
                onek=jnp.asarray(onek), oneq=jnp.asarray(oneq),
                tril=jnp.asarray(tril, BF16), tri_suffix=jnp.asarray(tri_suffix, BF16),
                triu=jnp.asarray(tril.T, BF16),
                trit=jnp.asarray(tril, BF16))


def _layer_weights(l, norm_pre, norm_post, w_in, kv_norm, w_uk, w_uv, forget_bias, out_norm, w_out):
    w = w_in[l]
    seg = lambda i: w[:, IN_OFFS[i]:IN_OFFS[i + 1]]
    a_qn, a_qr, a_ckv, a_kr, b_q, b_k, b_v, b_f, c_q, c_k, c_v, gate = [seg(i) for i in range(12)]
    zc = lambda n: jnp.zeros((D_MODEL, n), F32)
    qa_cols = []
    for h in range(A_HEADS):
        qa_cols += [a_qn[:, h * A_NOPE:(h + 1) * A_NOPE], a_qr[:, h * A_ROPE:(h + 1) * A_ROPE],
                    zc(HEAD_PAD - A_NOPE - A_ROPE)]
    grp = jnp.concatenate([a_kr, b_f, zc(LANES - A_ROPE - B_HEADS)], axis=1)
    wtok = jnp.concatenate(qa_cols + [a_ckv, grp, b_q, b_k, b_v, c_q, c_k, c_v, gate], axis=1).astype(BF16)
    wvt = jnp.concatenate([b_v, c_v], axis=1).T.astype(BF16)
    uk = w_uk[l]
    uv = w_uv[l]
    wukp = jnp.concatenate(
        [jnp.concatenate([uk[:, h, :], jnp.zeros((A_KV_RANK, HEAD_PAD - A_NOPE), F32)], axis=1)
         for h in range(A_HEADS)], axis=1).astype(BF16)
    wuvt = uv.reshape(A_KV_RANK, A_WIDTH).T.astype(BF16)
    mabs = []
    eye = jnp.eye(A_ROPE, dtype=F32)
    for h in range(A_HEADS):
        top = jnp.concatenate([uk[:, h, :].T, jnp.zeros((A_NOPE, LANES), F32)], axis=1)
        mid = jnp.concatenate([jnp.zeros((A_ROPE, A_KV_RANK), F32), eye, jnp.zeros((A_ROPE, LANES - A_ROPE), F32)],
                              axis=1)
        bot = jnp.zeros((HEAD_PAD - A_NOPE - A_ROPE, A_KV_RANK + LANES), F32)
        mabs.append(jnp.concatenate([top, mid, bot], axis=0))
    fb = jnp.concatenate([jnp.zeros((GRP_F,), F32), forget_bias[l], jnp.zeros((LANES - GRP_F - B_HEADS,), F32)])
    return dict(npre=norm_pre[l][None, :], npost=norm_post[l][None, :], wtok=wtok, wvt=wvt,
                kvn=kv_norm[l][None, :], wukp=wukp, wuvt=wuvt, fb=fb[None, :],
                mabs=jnp.stack(mabs).astype(BF16), wuv=jnp.transpose(uv, (1, 0, 2)).astype(BF16),
                onorm=out_norm[l][None, :], wout=w_out[l].astype(BF16))


def kernel(x_prompt, x_sample, cache_mla_ckv, cache_mla_kpe, cache_fox_k, cache_fox_v, cache_fox_logf,
           cache_sb_k, cache_sb_v, norm_pre, norm_post, w_in, mla_kv_norm, mla_w_uk, mla_w_uv,
           fox_forget_bias, out_norm, w_out):
    batch, seq, _ = x_prompt.shape
    dec_batch, t_new, _ = x_sample.shape
    depth = w_in.shape[0]
    past = cache_mla_ckv.shape[2]
    assert seq % ATT_BLOCK == 0 and past % ROW_TILE == 0 and (dec_batch * t_new) % ROW_TILE == 0
    assert ROW_TILE % t_new == 0 and t_new <= LANES

    consts = _constants()
    tabs_p = _rope_tables(jnp.arange(seq, dtype=jnp.int32))
    tabs_s = _rope_tables(jnp.tile(past + jnp.arange(t_new, dtype=jnp.int32), ROW_TILE // t_new))
    key_len = past + LANES
    pad_len = -(-key_len // ROW_TILE) * ROW_TILE
    logf_cache_t = jnp.transpose(cache_fox_logf, (0, 1, 3, 2))

    xp = x_prompt.reshape(batch * seq, D_MODEL)
    xs = x_sample.reshape(dec_batch * t_new, D_MODEL)
    rows_p, rows_s = [], []
    for l in range(depth):
        wl = _layer_weights(l, norm_pre, norm_post, w_in, mla_kv_norm, mla_w_uk, mla_w_uv, fox_forget_bias,
                            out_norm, w_out)
        wl.update(sel=consts['sel'], place=consts['place'])

        (ckv, kpe, bk, bv, logf, ck, cv, grp, qa, ka, qb, kb, qc, kc, _, _, gate, vta, vtbc) = _project(
            xp, wl, tabs_p, seq // ROW_TILE)
        qb_aug, kb_aug = _fox_augment(grp, qb, kb, consts, batch, seq)
        wide = 4 * HEAD_PAD
        ns = seq // ROW_TILE
        r3 = lambda a: a.reshape(batch, seq, wide)
        vta4 = vta.reshape(batch, ns, A_WIDTH, ROW_TILE)
        vtbc4 = vtbc.reshape(batch, ns, B_WIDTH + C_WIDTH, ROW_TILE)
        oa = _attn_softmax(r3(qa), r3(ka), vta4, batch=batch, seq=seq, heads_per_step=1, dv=A_V,
                           vt_row_block0=0, causal_kind="chunk", name="attn_mla")
        ob = _attn_softmax(r3(qb_aug), r3(kb_aug), vtbc4, batch=batch, seq=seq, heads_per_step=2, dv=B_HEAD_DIM,
                           vt_row_block0=0, causal_kind="causal", name="attn_fox")
        oc = _attn_stick(r3(qc), r3(kc), vtbc4, consts['tri_suffix'], batch=batch, seq=seq,
                         vt_row_block0=B_WIDTH // (2 * C_HEAD_DIM))
        xp = _merge(oa.reshape(-1, A_WIDTH), ob.reshape(-1, B_WIDTH), oc.reshape(-1, C_WIDTH), gate, wl, xp)
        rows_p.append((ckv.reshape(batch, seq, A_KV_RANK), kpe.reshape(batch, seq, A_ROPE),
                       bk.reshape(batch, seq, B_HEADS, B_HEAD_DIM), bv.reshape(batch, seq, B_HEADS, B_HEAD_DIM),
                       logf.reshape(batch, seq, B_HEADS),
                       ck.reshape(batch, seq, C_HEADS, C_HEAD_DIM), cv.reshape(batch, seq, C_HEADS, C_HEAD_DIM)))

        (ckv, kpe, bk, bv, logf, ck, cv, _, qa, _, _, _, _, _, qbu, qcu, gate, _, _) = _project(
            xs, wl, tabs_s, 1)
        s3 = lambda a: a.reshape(dec_batch, t_new, a.shape[-1])
        logf_new_t = jnp.transpose(s3(logf), (0, 2, 1))
        logf_all = jnp.concatenate(
            [logf_cache_t[l], logf_new_t, jnp.zeros((dec_batch, B_HEADS, pad_len - past - t_new), F32)], axis=2)
        f_all = _cumsum_rows(logf_all.reshape(dec_batch * B_HEADS, pad_len), consts['triu'])
        f_all = f_all.reshape(dec_batch, B_HEADS, pad_len)
        fq = f_all[:, :, past:past + t_new].reshape(dec_batch, B_HEADS * t_new, 1)
        caches = (cache_mla_ckv[l], cache_mla_kpe[l],
                  cache_fox_k[l].reshape(dec_batch, past, B_WIDTH), cache_fox_v[l].reshape(dec_batch, past, B_WIDTH),
                  cache_sb_k[l].reshape(dec_batch, past, C_WIDTH), cache_sb_v[l].reshape(dec_batch, past, C_WIDTH))
        news = tuple(s3(a) for a in (ckv, kpe, bk, bv, ck, cv))
        oa, ob, oc = _sample_attn(s3(qa), s3(qbu), s3(qcu), fq, f_all[:, :, :key_len], caches, news, wl,
                                  consts['trit'], dec_batch=dec_batch, t_new=t_new, past=past)
        xs = _merge(oa.reshape(-1, A_WIDTH), ob.reshape(-1, B_WIDTH), oc.reshape(-1, C_WIDTH), gate, wl, xs)
        rows_s.append((s3(ckv), s3(kpe), s3(bk).reshape(dec_batch, t_new, B_HEADS, B_HEAD_DIM),
                       s3(bv).reshape(dec_batch, t_new, B_HEADS, B_HEAD_DIM), s3(logf),
                       s3(ck).reshape(dec_batch, t_new, C_HEADS, C_HEAD_DIM),
                       s3(cv).reshape(dec_batch, t_new, C_HEADS, C_HEAD_DIM)))

    st = lambda rows, i: jnp.stack([r[i] for r in rows], axis=0)
    return ((xp.reshape(batch, seq, D_MODEL), xs.reshape(dec_batch, t_new, D_MODEL))
            + tuple(st(rows_p, i) for i in range(7)) + tuple(st(rows_s, i) for i in range(7)))
```

```python
import functools
import math

import numpy as np
import jax
import jax.numpy as jnp
from jax import lax
from jax.experimental import pallas as pl
from jax.experimental.pallas import tpu as pltpu

F32 = jnp.float32
BF16 = jnp.bfloat16

D_MODEL = 1024
CHUNK = 64
A_HEADS, A_NOPE, A_ROPE, A_V, A_KV_RANK = 4, 64, 32, 128, 256
B_HEADS, B_HEAD_DIM = 4, 64
C_HEADS, C_HEAD_DIM = 4, 64
ROPE_THETA = 10000.0
A_WIDTH = A_HEADS * A_V
B_WIDTH = B_HEADS * B_HEAD_DIM
C_WIDTH = C_HEADS * C_HEAD_DIM
D_MIX = A_WIDTH + B_WIDTH + C_WIDTH
A_SCALE = (A_NOPE + A_ROPE) ** -0.5
B_SCALE = B_HEAD_DIM ** -0.5
C_SCALE = C_HEAD_DIM ** -0.5
IN_SIZES = (A_HEADS * A_NOPE, A_HEADS * A_ROPE, A_KV_RANK, A_ROPE,
            B_WIDTH, B_WIDTH, B_WIDTH, B_HEADS, C_WIDTH, C_WIDTH, C_WIDTH, D_MIX)
IN_OFFS = tuple(int(v) for v in np.concatenate([[0], np.cumsum(IN_SIZES)]))
NORM_EPS = 1e-6
NEG_INF = -1e30
LOG2E = math.log2(math.e)

LANES = 128
HEAD_PAD = LANES
ROW_TILE = 256
ATT_BLOCK = 512
MERGE_TILE = 512
VMEM_LIMIT = 48 * 1024 * 1024

AUG_NEG_F = B_HEAD_DIM
AUG_ONE = B_HEAD_DIM + 3
GRP_KR = 0
GRP_F = A_ROPE


def _dot(a, b):
    return jnp.dot(a, b, preferred_element_type=F32)


def _dot_nt(a, b):
    return lax.dot_general(a, b, (((1,), (1,)), ((), ())), preferred_element_type=F32)


def _split2(x):
    hi = x.astype(BF16)
    lo = (x - hi.astype(F32)).astype(BF16)
    return hi, lo


def _split3(x):
    h1 = x.astype(BF16)
    r1 = x - h1.astype(F32)
    h2 = r1.astype(BF16)
    h3 = (r1 - h2.astype(F32)).astype(BF16)
    return h1, h2, h3


def _div_pow2(x, n):
    assert n & (n - 1) == 0
    return jnp.right_shift(x, n.bit_length() - 1)


def _mod_pow2(x, n):
    assert n & (n - 1) == 0
    return jnp.bitwise_and(x, n - 1)


def _log2_keep(z2):
    nz = -z2
    return jnp.minimum(nz, 0.0) - jnp.log2(1.0 + jnp.exp2(jnp.minimum(z2, nz)))


def _rope_lanes(x, cos, sin_lo, sin_hi):
    return x * cos + pltpu.roll(x, LANES - A_ROPE // 2, 1) * sin_lo + pltpu.roll(x, A_ROPE // 2, 1) * sin_hi


def _proj_kernel(x_ref, npre_ref, wtok_ref, wvt_ref, kvn_ref, wukp_ref, wuvt_ref, sel_ref, place_ref, fb_ref,
                 cosa_ref, sla_ref, sha_ref, cosg_ref, slg_ref, shg_ref,
                 ckv_o, kpe_o, bk_o, bv_o, logf_o, ck_o, cv_o, grp_o,
                 qa_o, ka_o, qb_o, kb_o, qc_o, kc_o, qbu_o, qcu_o, gate_o, vta_o, vtbc_o):
    x = x_ref[...]
    ms = jnp.mean(x * x, axis=-1, keepdims=True)
    h = (x * lax.rsqrt(ms + NORM_EPS) * npre_ref[...]).astype(BF16)

    col = [0]

    def mm(width):
        a = col[0]
        col[0] = a + width
        return _dot(h, wtok_ref[:, a:a + width])

    qa = mm(A_HEADS * HEAD_PAD) * (A_SCALE * LOG2E)
    cosa, sla, sha = cosa_ref[...], sla_ref[...], sha_ref[...]
    qa_o[...] = jnp.concatenate(
        [_rope_lanes(qa[:, g * HEAD_PAD:(g + 1) * HEAD_PAD], cosa, sla, sha) for g in range(A_HEADS)],
        axis=1).astype(BF16)

    ckv = mm(A_KV_RANK)
    ckvn = ckv * lax.rsqrt(jnp.mean(ckv * ckv, axis=-1, keepdims=True) + NORM_EPS) * kvn_ref[...]
    ckv_o[...] = ckvn
    cb = ckvn.astype(BF16)

    grp = mm(LANES)
    roped = _rope_lanes(grp, cosg_ref[...], slg_ref[...], shg_ref[...])
    xf = grp + fb_ref[...]
    logsig = jnp.minimum(xf, 0.0) - jnp.log1p(jnp.exp(-jnp.abs(xf)))
    lane = lax.broadcasted_iota(jnp.int32, grp.shape, 1)
    grp2 = jnp.where(lane < GRP_F, roped, jnp.where(lane < GRP_F + B_HEADS, logsig, 0.0))
    grp_o[...] = grp2
    kpe_o[...] = grp2[:, GRP_KR:GRP_KR + A_ROPE]
    logf_o[...] = grp2[:, GRP_F:GRP_F + B_HEADS]

    ka_o[...] = (_dot(cb, wukp_ref[...]) + _dot(grp2.astype(BF16), sel_ref[...])).astype(BF16)

    bq = (mm(B_WIDTH) * (B_SCALE * LOG2E)).astype(BF16)
    bk = mm(B_WIDTH)
    bv = mm(B_WIDTH)
    cq = (mm(C_WIDTH) * (C_SCALE * LOG2E)).astype(BF16)
    ck = mm(C_WIDTH)
    cv = mm(C_WIDTH)
    gate = mm(D_MIX)
    bk_o[...] = bk
    bv_o[...] = bv
    ck_o[...] = ck
    cv_o[...] = cv
    qbu_o[...] = bq
    qcu_o[...] = cq
    place = place_ref[...]
    qb_o[...] = _dot(bq, place).astype(BF16)
    kb_o[...] = _dot(bk.astype(BF16), place).astype(BF16)
    qc_o[...] = _dot(cq, place).astype(BF16)
    kc_o[...] = _dot(ck.astype(BF16), place).astype(BF16)
    gate_o[...] = gate.astype(BF16)

    vta_o[0] = _dot_nt(wuvt_ref[...], cb).astype(BF16)
    vtbc_o[0] = _dot_nt(wvt_ref[...], h).astype(BF16)


def _project(x2d, wl, tabs, n_tab_tiles):
    n = x2d.shape[0]
    tm = ROW_TILE
    nt = n // tm
    row = lambda w: pl.BlockSpec((tm, w), lambda i: (i, 0))
    full = lambda a: pl.BlockSpec(a.shape, lambda i: (0,) * a.ndim)
    tab = pl.BlockSpec((tm, LANES), lambda i: (i % n_tab_tiles, 0))
    weights = (wl['npre'], wl['wtok'], wl['wvt'], wl['kvn'], wl['wukp'], wl['wuvt'], wl['sel'], wl['place'], wl['fb'])
    in_specs = [row(D_MODEL)] + [full(w) for w in weights] + [tab] * 6
    f32o = lambda w: jax.ShapeDtypeStruct((n, w), F32)
    bfo = lambda w: jax.ShapeDtypeStruct((n, w), BF16)
    wide = A_HEADS * HEAD_PAD
    out_shape = (f32o(A_KV_RANK), f32o(A_ROPE), f32o(B_WIDTH), f32o(B_WIDTH), f32o(B_HEADS), f32o(C_WIDTH),
                 f32o(C_WIDTH), f32o(LANES),
                 bfo(wide), bfo(wide), bfo(wide), bfo(wide), bfo(wide), bfo(wide), bfo(B_WIDTH), bfo(C_WIDTH),
                 bfo(D_MIX),
                 jax.ShapeDtypeStruct((nt, A_WIDTH, tm), BF16),
                 jax.ShapeDtypeStruct((nt, B_WIDTH + C_WIDTH, tm), BF16))
    out_specs = ([row(s.shape[1]) for s in out_shape[:17]]
                 + [pl.BlockSpec((1, A_WIDTH, tm), lambda i: (i, 0, 0)),
                    pl.BlockSpec((1, B_WIDTH + C_WIDTH, tm), lambda i: (i, 0, 0))])
    return pl.pallas_call(
        _proj_kernel, grid=(nt,), in_specs=in_specs, out_specs=out_specs, out_shape=out_shape,
        compiler_params=pltpu.CompilerParams(dimension_semantics=("parallel",), vmem_limit_bytes=VMEM_LIMIT),
        name="proj",
    )(x2d, *weights, *tabs)


def _fprep_kernel(grp_ref, qb_ref, kb_ref, tril_ref, pk_ref, pq_ref, onek_ref, oneq_ref, qb_o, kb_o, carry_sc):
    @pl.when(pl.program_id(1) == 0)
    def _():
        carry_sc[...] = jnp.zeros_like(carry_sc)

    g = grp_ref[...]
    lane = lax.broadcasted_iota(jnp.int32, g.shape, 1)
    g = jnp.where((lane >= GRP_F) & (lane < GRP_F + B_HEADS), g, 0.0)
    tril = tril_ref[...]
    g1, g2, g3 = _split3(g)
    f = _dot(tril, g1) + _dot(tril, g2) + _dot(tril, g3) + carry_sc[0:1, :]
    ts = g.shape[0]
    carry_sc[...] = jnp.broadcast_to(f[ts - 1:ts, :], carry_sc.shape)
    f1, f2, f3 = _split3(f * LOG2E)
    kb_o[...] = (kb_ref[...].astype(F32) + _dot(f1, pk_ref[0]) + _dot(f2, pk_ref[1]) + _dot(f3, pk_ref[2])
                 + onek_ref[...]).astype(BF16)
    qb_o[...] = (qb_ref[...].astype(F32) + _dot(f1, pq_ref[0]) + _dot(f2, pq_ref[1]) + _dot(f3, pq_ref[2])
                 + oneq_ref[...]).astype(BF16)


def _fox_augment(grp, qb, kb, consts, batch, seq):
    ts = ROW_TILE
    nt = seq // ts
    wide = B_HEADS * HEAD_PAD
    row = lambda w: pl.BlockSpec((ts, w), lambda b, t: (b * nt + t, 0))
    full = lambda a: pl.BlockSpec(a.shape, lambda b, t: (0,) * a.ndim)
    cs = (consts['tril'], consts['pk'], consts['pq'], consts['onek'], consts['oneq'])
    return pl.pallas_call(
        _fprep_kernel, grid=(batch, nt),
        in_specs=[row(LANES), row(wide), row(wide)] + [full(c) for c in cs],
        out_specs=[row(wide), row(wide)],
        out_shape=(jax.ShapeDtypeStruct(qb.shape, BF16), jax.ShapeDtypeStruct(kb.shape, BF16)),
        scratch_shapes=[pltpu.VMEM((8, LANES), F32)],
        compiler_params=pltpu.CompilerParams(dimension_semantics=("parallel", "arbitrary"),
                                             vmem_limit_bytes=VMEM_LIMIT),
        name="fox_augment",
    )(grp, qb, kb, *cs)


def _diag_iotas(blk):
    kk = lax.broadcasted_iota(jnp.int32, (blk, blk), 0)
    qq = lax.broadcasted_iota(jnp.int32, (blk, blk), 1)
    return kk, qq


def _key_block(k_ref, j, hh):
    blk = ATT_BLOCK
    return k_ref[0, pl.ds(pl.multiple_of(j * blk, blk), blk), hh * HEAD_PAD:(hh + 1) * HEAD_PAD]


def _values_times(vt_ref, j, row0, nrow, wb_ref_slab):
    nslab = ATT_BLOCK // ROW_TILE
    pv = None
    for s in range(nslab):
        d = _dot(vt_ref[0, j * nslab + s, row0:row0 + nrow, :], wb_ref_slab(s))
        pv = d if pv is None else pv + d
    return pv


def _attn_softmax_kernel(q_ref, k_ref, vt_ref, o_ref, s_sc, mx_sc, p_sc, al_sc, m_sc, l_sc, acc_sc,
                         *, heads, dv, causal_kind):
    qi = pl.program_id(2)
    m_sc[...] = jnp.full(m_sc.shape, NEG_INF, F32)
    l_sc[...] = jnp.zeros_like(l_sc)
    acc_sc[...] = jnp.zeros_like(acc_sc)
    p_sc[...] = jnp.zeros_like(p_sc)
    al_sc[...] = jnp.ones_like(al_sc)
    qs = [q_ref[0, :, hh * HEAD_PAD:(hh + 1) * HEAD_PAD] for hh in range(heads)]

    def scores(j, slot, mask):
        for hh in range(heads):
            st = _dot_nt(_key_block(k_ref, j, hh), qs[hh])
            if mask is not None:
                st = jnp.where(mask, st, NEG_INF)
            s_sc[slot, hh] = st
            mx_sc[slot, hh:hh + 1, :] = jnp.max(st, axis=0, keepdims=True)

    def weights(slot):
        for hh in range(heads):
            m_old = m_sc[hh:hh + 1, :]
            m_new = jnp.maximum(m_old, mx_sc[slot, hh:hh + 1, :])
            alpha = jnp.exp2(m_old - m_new)
            p = jnp.exp2(s_sc[slot, hh] - m_new)
            l_sc[hh:hh + 1, :] = alpha * l_sc[hh:hh + 1, :] + jnp.sum(p, axis=0, keepdims=True)
            p_sc[slot, hh] = p.astype(BF16)
            al_sc[slot, hh:hh + 1, :] = alpha
            m_sc[hh:hh + 1, :] = m_new

    def values(j, slot):
        for hh in range(heads):
            pv = _values_times(vt_ref, j, hh * dv, dv,
                               lambda s: p_sc[slot, hh, s * ROW_TILE:(s + 1) * ROW_TILE, :])
            acc_sc[hh * dv:(hh + 1) * dv, :] = al_sc[slot, hh:hh + 1, :] * acc_sc[hh * dv:(hh + 1) * dv, :] + pv

    kk, qq = _diag_iotas(ATT_BLOCK)
    if causal_kind == "chunk":
        dmask = _div_pow2(kk, CHUNK) <= _div_pow2(qq, CHUNK)
    else:
        dmask = kk <= qq
    scores(qi, 0, dmask)

    def trip(t, slot):
        scores(t - 1, slot, None)
        weights(1 - slot)
        values(jnp.where(t >= 3, t - 3, qi), slot)

    def body(i, c):
        trip(2 * i + 1, 1)
        trip(2 * i + 2, 0)
        return c

    lax.fori_loop(0, jnp.right_shift(qi, 1), body, 0)

    def finish(last):
        weights(last)
        values(jnp.where(qi >= 2, qi - 2, qi), 1 - last)
        values(jnp.where(qi >= 1, qi - 1, qi), last)

    odd = jnp.bitwise_and(qi, 1) == 1

    @pl.when(odd)
    def _():
        trip(qi, 1)
        finish(1)

    @pl.when(jnp.logical_not(odd))
    def _():
        finish(0)

    outs = []
    for hh in range(heads):
        outs.append(acc_sc[hh * dv:(hh + 1) * dv, :] / l_sc[hh:hh + 1, :])
    o_ref[0] = jnp.concatenate(outs, axis=0).T


def _attn_softmax(q, k, vt, *, batch, seq, heads_per_step, dv, vt_row_block0, causal_kind, name):
    blk = ATT_BLOCK
    nq = seq // blk
    ns = seq // ROW_TILE
    groups = 4 // heads_per_step
    hw = heads_per_step * HEAD_PAD
    rows = heads_per_step * dv
    kern = functools.partial(_attn_softmax_kernel, heads=heads_per_step, dv=dv, causal_kind=causal_kind)
    return pl.pallas_call(
        kern, grid=(batch, groups, nq),
        in_specs=[pl.BlockSpec((1, blk, hw), lambda b, g, i: (b, i, g)),
                  pl.BlockSpec((1, seq, hw), lambda b, g, i: (b, 0, g)),
                  pl.BlockSpec((1, ns, rows, ROW_TILE), lambda b, g, i: (b, 0, vt_row_block0 + g, 0))],
        out_specs=pl.BlockSpec((1, blk, rows), lambda b, g, i: (b, i, g)),
        out_shape=jax.ShapeDtypeStruct((batch, seq, 4 * dv), F32),
        scratch_shapes=[pltpu.VMEM((2, heads_per_step, blk, blk), F32),
                        pltpu.VMEM((2, heads_per_step, blk), F32),
                        pltpu.VMEM((2, heads_per_step, blk, blk), BF16),
                        pltpu.VMEM((2, heads_per_step, blk), F32),
                        pltpu.VMEM((heads_per_step, blk), F32), pltpu.VMEM((heads_per_step, blk), F32),
                        pltpu.VMEM((rows, blk), F32)],
        compiler_params=pltpu.CompilerParams(dimension_semantics=("parallel", "parallel", "arbitrary"),
                                             vmem_limit_bytes=VMEM_LIMIT),
        name=name,
    )(q, k, vt)


def _attn_stick_kernel(q_ref, k_ref, vt_ref, tri_ref, o_ref, z_sc, hi_sc, lo_sc, w_sc, carry_sc, acc_sc, *, heads):
    qi = pl.program_id(2)
    dv = C_HEAD_DIM
    carry_sc[...] = jnp.zeros_like(carry_sc)
    acc_sc[...] = jnp.zeros_like(acc_sc)
    w_sc[...] = jnp.zeros_like(w_sc)
    qs = [q_ref[0, :, hh * HEAD_PAD:(hh + 1) * HEAD_PAD] for hh in range(heads)]

    def logits(j, slot, mask):
        for hh in range(heads):
            zt = _dot_nt(_key_block(k_ref, j, hh), qs[hh])
            lk = _log2_keep(zt)
            if mask is not None:
                lk = jnp.where(mask, lk, 0.0)
                zt = jnp.where(mask, zt, NEG_INF)
            hi, lo = _split2(lk)
            z_sc[slot, hh] = zt
            hi_sc[slot, hh] = hi
            lo_sc[slot, hh] = lo

    def weights(slot):
        tri = tri_ref[...]
        sub = tri.shape[0]
        for hh in range(heads):
            carry = carry_sc[hh:hh + 1, :]
            for s in reversed(range(ATT_BLOCK // sub)):
                rows = slice(s * sub, (s + 1) * sub)
                cum = _dot(tri, hi_sc[slot, hh, rows, :]) + _dot(tri, lo_sc[slot, hh, rows, :])
                w_sc[slot, hh, rows, :] = jnp.exp2(z_sc[slot, hh, rows, :] + cum + carry).astype(BF16)
                carry = carry + cum[0:1, :]
            carry_sc[hh:hh + 1, :] = carry

    def values(j, slot):
        for hh in range(heads):
            pv = _values_times(vt_ref, j, hh * dv, dv,
                               lambda s: w_sc[slot, hh, s * ROW_TILE:(s + 1) * ROW_TILE, :])
            acc_sc[hh * dv:(hh + 1) * dv, :] += pv

    kk, qq = _diag_iotas(ATT_BLOCK)
    logits(qi, 0, kk < qq)

    def trip(t, slot):
        logits(qi - t, slot, None)
        weights(1 - slot)
        values(jnp.minimum(qi - t + 2, qi), slot)

    def body(i, c):
        trip(2 * i + 1, 1)
        trip(2 * i + 2, 0)
        return c

    lax.fori_loop(0, jnp.right_shift(qi, 1), body, 0)

    def finish(last):
        weights(last)
        values(jnp.minimum(1, qi), 1 - last)
        values(0, last)

    odd = jnp.bitwise_and(qi, 1) == 1

    @pl.when(odd)
    def _():
        trip(qi, 1)
        finish(1)

    @pl.when(jnp.logical_not(odd))
    def _():
        finish(0)

    o_ref[0] = acc_sc[...].T


def _attn_stick(q, k, vt, tri, *, batch, seq, vt_row_block0):
    blk = ATT_BLOCK
    nq = seq // blk
    ns = seq // ROW_TILE
    heads = 2
    hw = heads * HEAD_PAD
    rows = heads * C_HEAD_DIM
    kern = functools.partial(_attn_stick_kernel, heads=heads)
    return pl.pallas_call(
        kern, grid=(batch, C_HEADS // heads, nq),
        in_specs=[pl.BlockSpec((1, blk, hw), lambda b, g, i: (b, i, g)),
                  pl.BlockSpec((1, seq, hw), lambda b, g, i: (b, 0, g)),
                  pl.BlockSpec((1, ns, rows, ROW_TILE), lambda b, g, i: (b, 0, vt_row_block0 + g, 0)),
                  pl.BlockSpec(tri.shape, lambda b, g, i: (0, 0))],
        out_specs=pl.BlockSpec((1, blk, rows), lambda b, g, i: (b, i, g)),
        out_shape=jax.ShapeDtypeStruct((batch, seq, C_WIDTH), F32),
        scratch_shapes=[pltpu.VMEM((2, heads, blk, blk), F32),
                        pltpu.VMEM((2, heads, blk, blk), BF16),
                        pltpu.VMEM((2, heads, blk, blk), BF16),
                        pltpu.VMEM((2, heads, blk, blk), BF16),
                        pltpu.VMEM((heads, blk), F32), pltpu.VMEM((rows, blk), F32)],
        compiler_params=pltpu.CompilerParams(dimension_semantics=("parallel", "parallel", "arbitrary"),
                                             vmem_limit_bytes=VMEM_LIMIT),
        name="attn_stick",
    )(q, k, vt, tri)


def _merge_kernel(oa_ref, ob_ref, oc_ref, gate_ref, onorm_ref, wout_ref, x_ref, npost_ref, xo_ref):
    def gnorm(o):
        return o * lax.rsqrt(jnp.mean(o * o, axis=-1, keepdims=True) + NORM_EPS)

    normed = jnp.concatenate([gnorm(oa_ref[...]), gnorm(ob_ref[...]), gnorm(oc_ref[...])], axis=1) * onorm_ref[...]
    g = gate_ref[...].astype(F32)
    act = g / (1.0 + jnp.exp(-g))
    y = _dot((normed * act).astype(BF16), wout_ref[...])
    yn = y * lax.rsqrt(jnp.mean(y * y, axis=-1, keepdims=True) + NORM_EPS) * npost_ref[...]
    xo_ref[...] = x_ref[...] + yn


def _merge(oa, ob, oc, gate, wl, x2d):
    n = x2d.shape[0]
    tm = min(MERGE_TILE, n)
    row = lambda w: pl.BlockSpec((tm, w), lambda i: (i, 0))
    full = lambda a: pl.BlockSpec(a.shape, lambda i: (0,) * a.ndim)
    return pl.pallas_call(
        _merge_kernel, grid=(n // tm,),
        in_specs=[row(A_WIDTH), row(B_WIDTH), row(C_WIDTH), row(D_MIX), full(wl['onorm']), full(wl['wout']),
                  row(D_MODEL), full(wl['npost'])],
        out_specs=row(D_MODEL),
        out_shape=jax.ShapeDtypeStruct((n, D_MODEL), F32),
        compiler_params=pltpu.CompilerParams(dimension_semantics=("parallel",), vmem_limit_bytes=VMEM_LIMIT),
        name="merge",
    )(oa, ob, oc, gate, wl['onorm'], wl['wout'], x2d, wl['npost'])


def _cumsum_rows_kernel(x_ref, triu_ref, f_ref):
    blkw = triu_ref.shape[0]
    triu = triu_ref[...]
    carry = jnp.zeros((x_ref.shape[0], 1), F32)
    for c in range(x_ref.shape[1] // blkw):
        x1, x2, x3 = _split3(x_ref[:, c * blkw:(c + 1) * blkw])
        pref = _dot(x1, triu) + _dot(x2, triu) + _dot(x3, triu)
        f_ref[:, c * blkw:(c + 1) * blkw] = pref + carry
        carry = carry + pref[:, blkw - 1:blkw]


def _cumsum_rows(x, triu):
    return pl.pallas_call(
        _cumsum_rows_kernel, grid=(1,),
        in_specs=[pl.BlockSpec(x.shape, lambda i: (0, 0)), pl.BlockSpec(triu.shape, lambda i: (0, 0))],
        out_specs=pl.BlockSpec(x.shape, lambda i: (0, 0)),
        out_shape=jax.ShapeDtypeStruct(x.shape, F32),
        compiler_params=pltpu.CompilerParams(vmem_limit_bytes=VMEM_LIMIT),
        name="cumsum_rows",
    )(x, triu)


def _pad_rows(a, n):
    return jnp.concatenate([a, jnp.zeros((n - a.shape[0], a.shape[1]), a.dtype)], axis=0)


def _sample_attn_kernel(qa_ref, qbu_ref, qcu_ref, fq_ref, fk_ref,
                        ckv_c, kpe_c, fk_c, fv_c, sk_c, sv_c,
                        ckv_n, kpe_n, bk_n, bv_n, ck_n, cv_n,
                        mabs_ref, wuv_ref, trit_ref,
                        oa_ref, ob_ref, oc_ref, kpc_sc, kpn_sc, *, past, t_new):
    heads = A_HEADS
    rows = heads * t_new
    nb = LANES
    row_i = lax.broadcasted_iota(jnp.int32, (rows, nb), 0)
    col_i = lax.broadcasted_iota(jnp.int32, (rows, nb), 1)
    q_t = _mod_pow2(row_i, t_new)
    valid_n = col_i < t_new
    q_pos = past + q_t
    k_pos = past + col_i

    qa = qa_ref[0]
    qcat = jnp.concatenate([_dot(qa[:, h * HEAD_PAD:(h + 1) * HEAD_PAD], mabs_ref[h]) for h in range(heads)], axis=0)
    qabs = qcat[:, :A_KV_RANK].astype(BF16)
    qrp = qcat[:, A_KV_RANK:].astype(BF16)
    ckvc = ckv_c[0].astype(BF16)
    kpc_sc[...] = jnp.zeros_like(kpc_sc)
    kpc_sc[:, 0:A_ROPE] = kpe_c[0]
    kpn_sc[...] = jnp.zeros_like(kpn_sc)
    kpn_sc[0:t_new, 0:A_ROPE] = kpe_n[0]
    ckvn = _pad_rows(ckv_n[0], nb).astype(BF16)
    s_c = _dot_nt(qabs, ckvc) + _dot_nt(qrp, kpc_sc[...].astype(BF16))
    s_n = _dot_nt(qabs, ckvn) + _dot_nt(qrp, kpn_sc[...].astype(BF16))
    s_n = jnp.where(valid_n & (_div_pow2(k_pos, CHUNK) <= _div_pow2(q_pos, CHUNK)), s_n, NEG_INF)
    m = jnp.maximum(jnp.max(s_c, axis=1, keepdims=True), jnp.max(s_n, axis=1, keepdims=True))
    p_c = jnp.exp2(s_c - m)
    p_n = jnp.exp2(s_n - m)
    l = jnp.sum(p_c, axis=1, keepdims=True) + jnp.sum(p_n, axis=1, keepdims=True)
    olat = ((_dot(p_c.astype(BF16), ckvc) + _dot(p_n.astype(BF16), ckvn)) / l).astype(BF16)
    oa_ref[0] = jnp.concatenate([_dot(olat[h * t_new:(h + 1) * t_new, :], wuv_ref[h]) for h in range(heads)], axis=1)

    rw = lax.broadcasted_iota(jnp.int32, (rows, B_WIDTH), 0)
    lw = lax.broadcasted_iota(jnp.int32, (rows, B_WIDTH), 1)
    head_sel = _div_pow2(rw, t_new) == _div_pow2(lw, B_HEAD_DIM)

    def block_diag(qu):
        return jnp.where(head_sel, jnp.concatenate([qu] * heads, axis=0), jnp.zeros((), qu.dtype))

    def pick_heads(ofull):
        lt = _div_pow2(lax.broadcasted_iota(jnp.int32, (t_new, B_WIDTH), 1), B_HEAD_DIM)
        out = jnp.zeros((t_new, B_WIDTH), F32)
        for h in range(heads):
            out = out + jnp.where(lt == h, ofull[h * t_new:(h + 1) * t_new, :], 0.0)
        return out

    qbd = block_diag(qbu_ref[0])
    fk = fk_ref[0]
    fk_rows = jnp.concatenate([jnp.broadcast_to(fk[h:h + 1, :], (t_new, fk.shape[1])) for h in range(heads)], axis=0)
    fq = fq_ref[0]
    fkc = fk_c[0].astype(BF16)
    bkn = _pad_rows(bk_n[0], nb).astype(BF16)
    s_c = _dot_nt(qbd, fkc) + (fq - fk_rows[:, :past]) * LOG2E
    s_n = _dot_nt(qbd, bkn) + (fq - fk_rows[:, past:past + nb]) * LOG2E
    s_n = jnp.where(valid_n & (k_pos <= q_pos), s_n, NEG_INF)
    m = jnp.maximum(jnp.max(s_c, axis=1, keepdims=True), jnp.max(s_n, axis=1, keepdims=True))
    p_c = jnp.exp2(s_c - m)
    p_n = jnp.exp2(s_n - m)
    l = jnp.sum(p_c, axis=1, keepdims=True) + jnp.sum(p_n, axis=1, keepdims=True)
    ofull = (_dot(p_c.astype(BF16), fv_c[0].astype(BF16))
             + _dot(p_n.astype(BF16), _pad_rows(bv_n[0], nb).astype(BF16))) / l
    ob_ref[0] = pick_heads(ofull)

    qcd = block_diag(qcu_ref[0])
    trit = trit_ref[...]
    tw = trit.shape[0]
    z_c = _dot_nt(qcd, sk_c[0].astype(BF16))
    z_n = _dot_nt(qcd, _pad_rows(ck_n[0], nb).astype(BF16))
    strict_n = valid_n & (k_pos < q_pos)
    lk_n = jnp.where(strict_n, _log2_keep(z_n), 0.0)
    hi, lo = _split2(lk_n)
    cum_n = _dot(hi, trit[:nb, :nb]) + _dot(lo, trit[:nb, :nb])
    w_n = jnp.where(strict_n, jnp.exp2(z_n + cum_n), 0.0)
    lk_c = _log2_keep(z_c)
    ncb = past // tw
    cums = []
    for c in range(ncb):
        hi, lo = _split2(lk_c[:, c * tw:(c + 1) * tw])
        cums.append(_dot(hi, trit) + _dot(lo, trit))
    carry = cum_n[:, 0:1]
    w_blocks = [None] * ncb
    for c in range(ncb - 1, -1, -1):
        w_blocks[c] = jnp.exp2(z_c[:, c * tw:(c + 1) * tw] + cums[c] + carry)
        carry = carry + cums[c][:, 0:1]
    w_c = jnp.concatenate(w_blocks, axis=1)
    ofull = (_dot(w_c.astype(BF16), sv_c[0].astype(BF16))
             + _dot(w_n.astype(BF16), _pad_rows(cv_n[0], nb).astype(BF16)))
    oc_ref[0] = pick_heads(ofull)


def _sample_attn(qa, qbu, qcu, fq, fk, caches, news, wl, trit, *, dec_batch, t_new, past):
    def blk3(a):
        return pl.BlockSpec((1,) + a.shape[1:], lambda b: (b, 0, 0))

    full = lambda a: pl.BlockSpec(a.shape, lambda b: (0,) * a.ndim)
    ins = [qa, qbu, qcu, fq, fk] + list(caches) + list(news)
    consts = [wl['mabs'], wl['wuv'], trit]
    kern = functools.partial(_sample_attn_kernel, past=past, t_new=t_new)
    return pl.pallas_call(
        kern, grid=(dec_batch,),
        in_specs=[blk3(a) for a in ins] + [full(c) for c in consts],
        out_specs=[pl.BlockSpec((1, t_new, A_WIDTH), lambda b: (b, 0, 0)),
                   pl.BlockSpec((1, t_new, B_WIDTH), lambda b: (b, 0, 0)),
                   pl.BlockSpec((1, t_new, C_WIDTH), lambda b: (b, 0, 0))],
        out_shape=(jax.ShapeDtypeStruct((dec_batch, t_new, A_WIDTH), F32),
                   jax.ShapeDtypeStruct((dec_batch, t_new, B_WIDTH), F32),
                   jax.ShapeDtypeStruct((dec_batch, t_new, C_WIDTH), F32)),
        scratch_shapes=[pltpu.VMEM((past, LANES), F32), pltpu.VMEM((LANES, LANES), F32)],
        compiler_params=pltpu.CompilerParams(dimension_semantics=("parallel",), vmem_limit_bytes=VMEM_LIMIT),
        name="sample_attn",
    )(*ins, *consts)


def _rope_tables(pos):
    half = A_ROPE // 2
    inv = ROPE_THETA ** (-jnp.arange(half, dtype=F32) / half)
    ang = pos.astype(F32)[:, None] * inv[None, :]
    cos, sin = jnp.cos(ang), jnp.sin(ang)
    n = pos.shape[0]
    z = lambda w: jnp.zeros((n, w), F32)
    o = lambda w: jnp.ones((n, w), F32)
    cosa = jnp.concatenate([o(A_NOPE), cos, cos, o(HEAD_PAD - A_NOPE - A_ROPE)], axis=1)
    sla = jnp.concatenate([z(A_NOPE), -sin, z(half), z(HEAD_PAD - A_NOPE - A_ROPE)], axis=1)
    sha = jnp.concatenate([z(A_NOPE), z(half), sin, z(HEAD_PAD - A_NOPE - A_ROPE)], axis=1)
    cosg = jnp.concatenate([cos, cos, o(LANES - A_ROPE)], axis=1)
    slg = jnp.concatenate([-sin, z(half), z(LANES - A_ROPE)], axis=1)
    shg = jnp.concatenate([z(half), sin, z(LANES - A_ROPE)], axis=1)
    return cosa, sla, sha, cosg, slg, shg


def _constants():
    wide = 4 * HEAD_PAD
    place = np.zeros((B_WIDTH, wide), np.float32)
    for h in range(4):
        for d in range(B_HEAD_DIM):
            place[h * B_HEAD_DIM + d, h * HEAD_PAD + d] = 1.0
    sel = np.zeros((LANES, wide), np.float32)
    for h in range(A_HEADS):
        for j in range(A_ROPE):
            sel[GRP_KR + j, h * HEAD_PAD + A_NOPE + j] = 1.0
    pk = np.zeros((3, LANES, wide), np.float32)
    pq = np.zeros((3, LANES, wide), np.float32)
    onek = np.zeros((1, wide), np.float32)
    oneq = np.zeros((1, wide), np.float32)
    for h in range(B_HEADS):
        for i in range(3):
            pk[i, GRP_F + h, h * HEAD_PAD + AUG_NEG_F + i] = -1.0
            pq[i, GRP_F + h, h * HEAD_PAD + AUG_ONE + i] = 1.0
            onek[0, h * HEAD_PAD + AUG_ONE + i] = 1.0
            oneq[0, h * HEAD_PAD + AUG_NEG_F + i] = 1.0
    r = np.arange(ROW_TILE)
    tril = (r[:, None] >= r[None, :]).astype(np.float32)
    tri_suffix = (r[None, :] >= r[:, None]).astype(np.float32)
    return dict(place=jnp.asarray(place, BF16), sel=jnp.asarray(sel, BF16),
                pk=jnp.asarray(pk, BF16), pq=jnp.asarray(pq, BF16),
                onek=jnp.asarray(onek), oneq=jnp.asarray(oneq),
                tril=jnp.asarray(tril, BF16), tri_suffix=jnp.asarray(tri_suffix, BF16),
                triu=jnp.asarray(tril.T, BF16),
                trit=jnp.asarray(tril, BF16))


def _layer_weights(l, norm_pre, norm_post, w_in, kv_norm, w_uk, w_uv, forget_bias, out_norm, w_out):
    w = w_in[l]
    seg = lambda i: w[:, IN_OFFS[i]:IN_OFFS[i + 1]]
    a_qn, a_qr, a_ckv, a_kr, b_q, b_k, b_v, b_f, c_q, c_k, c_v, gate = [seg(i) for i in range(12)]
    zc = lambda n: jnp.zeros((D_MODEL, n), F32)
    qa_cols = []
    for h in range(A_HEADS):
        qa_cols += [a_qn[:, h * A_NOPE:(h + 1) * A_NOPE], a_qr[:, h * A_ROPE:(h + 1) * A_ROPE],
                    zc(HEAD_PAD - A_NOPE - A_ROPE)]
    grp = jnp.concatenate([a_kr, b_f, zc(LANES - A_ROPE - B_HEADS)], axis=1)
    wtok = jnp.concatenate(qa_cols + [a_ckv, grp, b_q, b_k, b_v, c_q, c_k, c_v, gate], axis=1).astype(BF16)
    wvt = jnp.concatenate([b_v, c_v], axis=1).T.astype(BF16)
    uk = w_uk[l]
    uv = w_uv[l]
    wukp = jnp.concatenate(
        [jnp.concatenate([uk[:, h, :], jnp.zeros((A_KV_RANK, HEAD_PAD - A_NOPE), F32)], axis=1)
         for h in range(A_HEADS)], axis=1).astype(BF16)
    wuvt = uv.reshape(A_KV_RANK, A_WIDTH).T.astype(BF16)
    mabs = []
    eye = jnp.eye(A_ROPE, dtype=F32)
    for h in range(A_HEADS):
        top = jnp.concatenate([uk[:, h, :].T, jnp.zeros((A_NOPE, LANES), F32)], axis=1)
        mid = jnp.concatenate([jnp.zeros((A_ROPE, A_KV_RANK), F32), eye, jnp.zeros((A_ROPE, LANES - A_ROPE), F32)],
                              axis=1)
        bot = jnp.zeros((HEAD_PAD - A_NOPE - A_ROPE, A_KV_RANK + LANES), F32)
        mabs.append(jnp.concatenate([top, mid, bot], axis=0))
    fb = jnp.concatenate([jnp.zeros((GRP_F,), F32), forget_bias[l], jnp.zeros((LANES - GRP_F - B_HEADS,), F32)])
    return dict(npre=norm_pre[l][None, :], npost=norm_post[l][None, :], wtok=wtok, wvt=wvt,
                kvn=kv_norm[l][None, :], wukp=wukp, wuvt=wuvt, fb=fb[None, :],
                mabs=jnp.stack(mabs).astype(BF16), wuv=jnp.transpose(uv, (1, 0, 2)).astype(BF16),
                onorm=out_norm[l][None, :], wout=w_out[l].astype(BF16))


def kernel(x_prompt, x_sample, cache_mla_ckv, cache_mla_kpe, cache_fox_k, cache_fox_v, cache_fox_logf,
           cache_sb_k, cache_sb_v, norm_pre, norm_post, w_in, mla_kv_norm, mla_w_uk, mla_w_uv,
           fox_forget_bias, out_norm, w_out):
    batch, seq, _ = x_prompt.shape
    dec_batch, t_new, _ = x_sample.shape
    depth = w_in.shape[0]
    past = cache_mla_ckv.shape[2]
    assert seq % ATT_BLOCK == 0 and past % ROW_TILE == 0 and (dec_batch * t_new) % ROW_TILE == 0
    assert ROW_TILE % t_new == 0 and t_new <= LANES

    consts = _constants()
    tabs_p = _rope_tables(jnp.arange(seq, dtype=jnp.int32))
    tabs_s = _rope_tables(jnp.tile(past + jnp.arange(t_new, dtype=jnp.int32), ROW_TILE // t_new))
    key_len = past + LANES
    pad_len = -(-key_len // ROW_TILE) * ROW_TILE
    logf_cache_t = jnp.transpose(cache_fox_logf, (0, 1, 3, 2))

    xp = x_prompt.reshape(batch * seq, D_MODEL)
    xs = x_sample.reshape(dec_batch * t_new, D_MODEL)
    rows_p, rows_s = [], []
    for l in range(depth):
        wl = _layer_weights(l, norm_pre, norm_post, w_in, mla_kv_norm, mla_w_uk, mla_w_uv, fox_forget_bias,
                            out_norm, w_out)
        wl.update(sel=consts['sel'], place=consts['place'])

        (ckv, kpe, bk, bv, logf, ck, cv, grp, qa, ka, qb, kb, qc, kc, _, _, gate, vta, vtbc) = _project(
            xp, wl, tabs_p, seq // ROW_TILE)
        qb_aug, kb_aug = _fox_augment(grp, qb, kb, consts, batch, seq)
        wide = 4 * HEAD_PAD
        ns = seq // ROW_TILE
        r3 = lambda a: a.reshape(batch, seq, wide)
        vta4 = vta.reshape(batch, ns, A_WIDTH, ROW_TILE)
        vtbc4 = vtbc.reshape(batch, ns, B_WIDTH + C_WIDTH, ROW_TILE)
        oa = _attn_softmax(r3(qa), r3(ka), vta4, batch=batch, seq=seq, heads_per_step=1, dv=A_V,
                           vt_row_block0=0, causal_kind="chunk", name="attn_mla")
        ob = _attn_softmax(r3(qb_aug), r3(kb_aug), vtbc4, batch=batch, seq=seq, heads_per_step=2, dv=B_HEAD_DIM,
                           vt_row_block0=0, causal_kind="causal", name="attn_fox")
        oc = _attn_stick(r3(qc), r3(kc), vtbc4, consts['tri_suffix'], batch=batch, seq=seq,
                         vt_row_block0=B_WIDTH // (2 * C_HEAD_DIM))
        xp = _merge(oa.reshape(-1, A_WIDTH), ob.reshape(-1, B_WIDTH), oc.reshape(-1, C_WIDTH), gate, wl, xp)
        rows_p.append((ckv.reshape(batch, seq, A_KV_RANK), kpe.reshape(batch, seq, A_ROPE),
                       bk.reshape(batch, seq, B_HEADS, B_HEAD_DIM), bv.reshape(batch, seq, B_HEADS, B_HEAD_DIM),
                       logf.reshape(batch, seq, B_HEADS),
                       ck.reshape(batch, seq, C_HEADS, C_HEAD_DIM), cv.reshape(batch, seq, C_HEADS, C_HEAD_DIM)))

        (ckv, kpe, bk, bv, logf, ck, cv, _, qa, _, _, _, _, _, qbu, qcu, gate, _, _) = _project(
            xs, wl, tabs_s, 1)
        s3 = lambda a: a.reshape(dec_batch, t_new, a.shape[-1])
        logf_new_t = jnp.transpose(s3(logf), (0, 2, 1))
        logf_all = jnp.concatenate(
            [logf_cache_t[l], logf_new_t, jnp.zeros((dec_batch, B_HEADS, pad_len - past - t_new), F32)], axis=2)
        f_all = _cumsum_rows(logf_all.reshape(dec_batch * B_HEADS, pad_len), consts['triu'])
        f_all = f_all.reshape(dec_batch, B_HEADS, pad_len)
        fq = f_all[:, :, past:past + t_new].reshape(dec_batch, B_HEADS * t_new, 1)
        caches = (cache_mla_ckv[l], cache_mla_kpe[l],
                  cache_fox_k[l].reshape(dec_batch, past, B_WIDTH), cache_fox_v[l].reshape(dec_batch, past, B_WIDTH),
                  cache_sb_k[l].reshape(dec_batch, past, C_WIDTH), cache_sb_v[l].reshape(dec_batch, past, C_WIDTH))
        news = tuple(s3(a) for a in (ckv, kpe, bk, bv, ck, cv))
        oa, ob, oc = _sample_attn(s3(qa), s3(qbu), s3(qcu), fq, f_all[:, :, :key_len], caches, news, wl,
                                  consts['trit'], dec_batch=dec_batch, t_new=t_new, past=past)
        xs = _merge(oa.reshape(-1, A_WIDTH), ob.reshape(-1, B_WIDTH), oc.reshape(-1, C_WIDTH), gate, wl, xs)
        rows_s.append((s3(ckv), s3(kpe), s3(bk).reshape(dec_batch, t_new, B_HEADS, B_HEAD_DIM),
                       s3(bv).reshape(dec_batch, t_new, B_HEADS, B_HEAD_DIM), s3(logf),
                       s3(ck).reshape(dec_batch, t_new, C_HEADS, C_HEAD_DIM),
                       s3(cv).reshape(dec_batch, t_new, C_HEADS, C_HEAD_DIM)))

    st = lambda rows, i: jnp.stack([r[i] for r in rows], axis=0)
    return ((xp.reshape(batch, seq, D_MODEL), xs.reshape(dec_batch, t_new, D_MODEL))
            + tuple(st(rows_p, i) for i in range(7)) + tuple(st(rows_s, i) for i in range(7)))
```
